```python
import jax
import jax.numpy as jnp
from jax import lax
import numpy as np

D_MODEL = 2048
BATCH = 4
SEQ = 2048
DEPTH = 2
DEC_BATCH = 32
DEC_SEQ = 4
PAST_LEN = 8192
PAGE_SIZE = 128

N_EVEN = (DEPTH + 1) // 2
N_ODD = DEPTH // 2
EPS = 1e-6
NEG_INF = -1e30
H_A = 4
DK_A = D_MODEL // 8
DV_A = D_MODEL // 8
W_A = H_A * DV_A
RET_CHUNK = 128
ROPE_BASE = 10000.0
G_B = 4
DG_B = D_MODEL // 8
W_B = G_B * DG_B
CHUNK_B = 128
H_C = 8
HD_C = D_MODEL // 16
W_C = H_C * HD_C
Q_BLOCK = 128
H_D = 8
DK_D = D_MODEL // 16
DV_D = D_MODEL // 16
W_D = H_D * DV_D
HGRN_CHUNK = 64
N_KEYS = 128
N_EXPERTS = N_KEYS * N_KEYS
PEER_HEADS = 8
PEER_DK = 256
PEER_TOPK = 16
PEER_BLOCK = 128

IN_EVEN = 2 * H_A * DK_A + 2 * W_A + 2 * W_B
IN_ODD = 3 * W_C + H_C + 2 * H_D * DK_D + 2 * W_D

kernel_name = 'hybrid_retnet_gmlp_fox_hgrn2_peer_step'


def split_cols(z, widths):
    parts, start = [], 0
    for w in widths:
        parts.append(z[..., start:start + w])
        start += w
    return parts


def rms_norm(x, g):
    xf = x.astype(jnp.float32)
    y = xf * lax.rsqrt(jnp.mean(xf * xf, axis=-1, keepdims=True) + EPS)
    return (y * g.astype(jnp.float32)).astype(x.dtype)


def head_rms(x):
    xf = x.astype(jnp.float32)
    return xf * lax.rsqrt(jnp.mean(xf * xf, axis=-1, keepdims=True) + EPS)


def layer_norm(x, g, b):
    xf = x.astype(jnp.float32)
    mu = jnp.mean(xf, axis=-1, keepdims=True)
    var = jnp.mean(jnp.square(xf - mu), axis=-1, keepdims=True)
    return (xf - mu) * lax.rsqrt(var + EPS) * g.astype(jnp.float32) + b.astype(jnp.float32)


def rope(x, pos):
    half = x.shape[-1] // 2
    inv = ROPE_BASE ** (-jnp.arange(half, dtype=jnp.float32) / half)
    ang = pos.astype(jnp.float32)[:, None] * inv[None, :]
    cos = jnp.cos(ang)[None, :, None, :]
    sin = jnp.sin(ang)[None, :, None, :]
    xf = x.astype(jnp.float32)
    x1, x2 = xf[..., :half], xf[..., half:]
    return jnp.concatenate([x1 * cos - x2 * sin, x1 * sin + x2 * cos], axis=-1)


def scan_chunks(chunk_fn, S0, arrays, chunk):
    B, T = arrays[0].shape[:2]
    n = T // chunk
    xs = tuple(a.reshape(B, n, chunk, *a.shape[2:]).swapaxes(0, 1) for a in arrays)

    def step(S, inp):
        o, S = chunk_fn(S, *inp)
        return S, o

    S, o = lax.scan(step, S0, xs)
    return o.swapaxes(0, 1).reshape(B, T, *o.shape[3:]), S


def retention_chunk(S0, q, k, v):
    L = q.shape[1]
    lg = jnp.log1p(-jnp.exp2(-5.0 - jnp.arange(H_A, dtype=jnp.float32)))
    idx = jnp.arange(L, dtype=jnp.float32)
    rel = idx[:, None] - idx[None, :]
    dec = jnp.where(rel >= 0, jnp.exp(lg[:, None, None] * jnp.maximum(rel, 0.0)[None]), 0.0)
    qf, kf, vf = q.astype(jnp.float32), k.astype(jnp.float32), v.astype(jnp.float32)
    scores = jnp.einsum('bthd,bshd->bhts', qf, kf) * dec[None]
    intra = jnp.einsum('bhts,bshe->bthe', scores, vf)
    cross_dec = jnp.exp(lg[None, :] * (idx[:, None] + 1.0))
    cross = jnp.einsum('bthd,bhde->bthe', qf, S0) * cross_dec[None, :, :, None]
    k_dec = kf * jnp.exp(lg[None, :] * (L - 1.0 - idx[:, None]))[None, :, :, None]
    S = jnp.exp(lg * L)[None, :, None, None] * S0 + jnp.einsum('bshd,bshe->bhde', k_dec, vf)
    return intra + cross, S


def hgrn_chunk(S0, q, logf, k, v):
    L = q.shape[1]
    b = jnp.cumsum(logf, axis=1)
    causal = jnp.tril(jnp.ones((L, L), dtype=bool))[None, :, :, None, None]
    diff = b[:, :, None] - b[:, None, :]
    w = jnp.exp(jnp.where(causal, diff, -jnp.inf))
    A = jnp.einsum('bthc,bshc,btshc->bhts', q, k, w)
    o = jnp.einsum('bhts,bshe->bthe', A, v) + jnp.einsum('bthc,bhce->bthe', q * jnp.exp(b), S0)
    bL = b[:, -1]
    S = jnp.exp(bL)[..., None] * S0 + jnp.einsum('bshc,bshe->bhce', k * jnp.exp(bL[:, None] - b), v)
    return o, S


def fox_prompt(q, k, v, logf):
    B, T, H, D = q.shape
    nq = T // Q_BLOCK
    F = jnp.cumsum(logf, axis=1).transpose(0, 2, 1)
    kf, vf = k.astype(jnp.float32), v.astype(jnp.float32)
    qb = (q.astype(jnp.float32) * HD_C ** -0.5).reshape(B, nq, Q_BLOCK, H, D).swapaxes(0, 1)
    Fb = F.reshape(B, H, nq, Q_BLOCK).transpose(2, 0, 1, 3)
    kpos = jnp.arange(T)

    def block(args):
        i, qi, Fi = args
        s = jnp.einsum('bthd,bshd->bhts', qi, kf) + Fi[..., :, None] - F[..., None, :]
        qpos = i * Q_BLOCK + jnp.arange(Q_BLOCK)
        s = jnp.where(qpos[:, None] >= kpos[None, :], s, NEG_INF)
        p = jax.nn.softmax(s, axis=-1)
        return jnp.einsum('bhts,bshd->bthd', p, vf)

    o = lax.map(block, (jnp.arange(nq), qb, Fb))
    return o.swapaxes(0, 1).reshape(B, T, H, D)


def fox_sample(q, k, v, logf, cache_k, cache_v, cache_logf, page_table):
    B, L, H, D = q.shape
    n_pages = page_table.shape[1]
    qf = q.astype(jnp.float32) * HD_C ** -0.5
    Fn = jnp.cumsum(logf, axis=1).transpose(0, 2, 1)
    s = jnp.einsum('bthd,bshd->bhts', qf, k.astype(jnp.float32)) + Fn[..., :, None] - Fn[..., None, :]
    s = jnp.where(jnp.tril(jnp.ones((L, L), dtype=bool)), s, NEG_INF)
    m = jnp.max(s, axis=-1)
    p = jnp.exp(s - m[..., None])
    l = jnp.sum(p, axis=-1)
    acc = jnp.einsum('bhts,bshd->bhtd', p, v.astype(jnp.float32))
    lf_past = cache_logf[page_table].reshape(B, n_pages * PAGE_SIZE, H).astype(jnp.float32)
    suffix = lax.cumsum(lf_past, axis=1, reverse=True) - lf_past
    suffix = suffix.reshape(B, n_pages, PAGE_SIZE, H).transpose(1, 0, 3, 2)

    def step(carry, inp):
        m, l, acc = carry
        pid, suf = inp
        kp = cache_k[pid].astype(jnp.float32)
        vp = cache_v[pid].astype(jnp.float32)
        s = jnp.einsum('bthd,bshd->bhts', qf, kp) + Fn[..., :, None] + suf[..., None, :]
        m_new = jnp.maximum(m, jnp.max(s, axis=-1))
        corr = jnp.exp(m - m_new)
        p = jnp.exp(s - m_new[..., None])
        l = l * corr + jnp.sum(p, axis=-1)
        acc = acc * corr[..., None] + jnp.einsum('bhts,bshd->bhtd', p, vp)
        return (m_new, l, acc), None

    (m, l, acc), _ = lax.scan(step, (m, l, acc), (page_table.T, suffix))
    return (acc / l[..., None]).transpose(0, 2, 1, 3)


def even_mix(xn, ret_state, pos, is_prompt, w_in, w_out, ln_g, ln_b, w_s, b_s):
    B, L, _ = xn.shape
    z = xn @ w_in
    q, k, v, g, u_b, v_b = split_cols(z, (H_A * DK_A, H_A * DK_A, W_A, W_A, W_B, W_B))
    q = rope(q.reshape(B, L, H_A, DK_A), pos)
    k = rope(k.reshape(B, L, H_A, DK_A), pos) * DK_A ** -0.5
    v = v.reshape(B, L, H_A, DV_A)
    if is_prompt:
        S0 = jnp.zeros((B, H_A, DK_A, DV_A), jnp.float32)
        o, S = scan_chunks(retention_chunk, S0, (q, k, v), RET_CHUNK)
    else:
        o, S = retention_chunk(ret_state.astype(jnp.float32), q, k, v)
    o_a = head_rms(o) * jax.nn.silu(g.reshape(B, L, H_A, DV_A).astype(jnp.float32))
    u = jax.nn.gelu(u_b.astype(jnp.float32)).reshape(B, L, G_B, DG_B)
    vn = layer_norm(jax.nn.gelu(v_b.astype(jnp.float32)), ln_g, ln_b).reshape(B, L, G_B, DG_B)
    Lc = min(L, CHUNK_B)
    nC = L // Lc
    wm = jnp.tril(w_s[:, :Lc, :Lc].astype(jnp.float32))
    mix = jnp.einsum('gts,bcsgd->bctgd', wm, vn.reshape(B, nC, Lc, G_B, DG_B))
    mix = mix + b_s[:, :Lc].astype(jnp.float32).T[None, None, :, :, None]
    o_b = u * mix.reshape(B, L, G_B, DG_B)
    cat = jnp.concatenate([o_a.reshape(B, L, W_A), o_b.reshape(B, L, W_B)], axis=-1).astype(xn.dtype)
    return cat @ w_out, S, vn


def odd_mix(xn, fox_cache, hgrn_state, page_table, is_prompt, w_in, f_bias, lb, norm_g, w_out):
    B, L, _ = xn.shape
    z = xn @ w_in
    qc, kc, vc, fc, qd, fd, vd, gd = split_cols(z, (W_C, W_C, W_C, H_C, H_D * DK_D, H_D * DK_D, W_D, W_D))
    qc = qc.reshape(B, L, H_C, HD_C)
    kc = kc.reshape(B, L, H_C, HD_C)
    vc = vc.reshape(B, L, H_C, HD_C)
    logf_c = jax.nn.log_sigmoid(fc.astype(jnp.float32) + f_bias.astype(jnp.float32))
    if is_prompt:
        o_c = fox_prompt(qc, kc, vc, logf_c)
    else:
        o_c = fox_sample(qc, kc, vc, logf_c, fox_cache[0], fox_cache[1], fox_cache[2], page_table)
    fdf = fd.reshape(B, L, H_D, DK_D).astype(jnp.float32)
    lbh = lb.reshape(H_D, DK_D)
    logf_d = jnp.logaddexp(jnp.log(lbh), jnp.log1p(-lbh) + jax.nn.log_sigmoid(fdf))
    k_d = (1.0 - lbh) * jax.nn.sigmoid(-fdf)
    q_d = jax.nn.silu(qd.reshape(B, L, H_D, DK_D).astype(jnp.float32))
    v_d = vd.reshape(B, L, H_D, DV_D).astype(jnp.float32)
    if is_prompt:
        S0 = jnp.zeros((B, H_D, DK_D, DV_D), jnp.float32)
        o_d, S = scan_chunks(hgrn_chunk, S0, (q_d, logf_d, k_d, v_d), HGRN_CHUNK)
    else:
        o_d, S = hgrn_chunk(hgrn_state.astype(jnp.float32), q_d, logf_d, k_d, v_d)
    o_d = head_rms(o_d) * norm_g.astype(jnp.float32) * jax.nn.silu(gd.reshape(B, L, H_D, DV_D).astype(jnp.float32))
    cat = jnp.concatenate([o_c.reshape(B, L, W_C), o_d.reshape(B, L, W_D)], axis=-1).astype(xn.dtype)
    return cat @ w_out, S, kc, vc, logf_c


def peer_ffn(x, w_q, sub_keys, u_tab, v_tab):
    shape = x.shape
    xt = x.reshape(-1, shape[-1])
    T = xt.shape[0]
    pad = (-T) % PEER_BLOCK
    xt = jnp.pad(xt, ((0, pad), (0, 0)))
    nb = xt.shape[0] // PEER_BLOCK
    kk = PEER_TOPK * PEER_TOPK

    def block(xb):
        qh = (xb @ w_q).astype(jnp.float32).reshape(PEER_BLOCK, PEER_HEADS, 2, PEER_DK // 2)
        s = jnp.einsum('thpc,hpkc->thpk', qh, sub_keys.astype(jnp.float32))
        sv, si = lax.top_k(s, PEER_TOPK)
        cand = (sv[:, :, 0, :, None] + sv[:, :, 1, None, :]).reshape(PEER_BLOCK, PEER_HEADS, kk)
        cidx = (si[:, :, 0, :, None] * N_KEYS + si[:, :, 1, None, :]).reshape(PEER_BLOCK, PEER_HEADS, kk)
        top_s, sel = lax.top_k(cand, PEER_TOPK)
        eidx = jnp.take_along_axis(cidx, sel, axis=-1)
        gate = jax.nn.softmax(top_s, axis=-1)
        h = jax.nn.gelu(jnp.einsum('td,thkd->thk', xb, u_tab[eidx]).astype(jnp.float32))
        return jnp.einsum('thk,thkd->td', (gate * h).astype(xb.dtype), v_tab[eidx])

    y = lax.map(block, xt.reshape(nb, PEER_BLOCK, shape[-1]))
    return y.reshape(-1, shape[-1])[:T].reshape(shape)


def setup_inputs(seed: int = 0) -> dict:
    key = jax.random.key(seed)
    ks = jax.random.split(key, 32)
    f32 = jnp.float32
    n_pages = PAST_LEN // PAGE_SIZE
    n_used = DEC_BATCH * n_pages
    n_pool = n_used + max(1, n_used // 4)

    def nrm(k, shape, scale):
        return jax.random.normal(k, shape, f32) * scale

    page_table = jax.random.permutation(ks[0], n_pool)[:n_used].reshape(DEC_BATCH, n_pages).astype(jnp.int32)
    return {
        'x_prompt': nrm(ks[1], (BATCH, SEQ, D_MODEL), 1.0),
        'x_sample': nrm(ks[2], (DEC_BATCH, DEC_SEQ, D_MODEL), 1.0),
        'state_ret': nrm(ks[3], (N_EVEN, DEC_BATCH, H_A, DK_A, DV_A), 0.5),
        'cache_k': nrm(ks[4], (N_ODD, n_pool, PAGE_SIZE, H_C, HD_C), 1.0),
        'cache_v': nrm(ks[5], (N_ODD, n_pool, PAGE_SIZE, H_C, HD_C), 1.0),
        'cache_logf': jax.nn.log_sigmoid(8.0 + nrm(ks[6], (N_ODD, n_pool, PAGE_SIZE, H_C), 0.5)),
        'state_hgrn': nrm(ks[7], (N_ODD, DEC_BATCH, H_D, DK_D, DV_D), 0.5),
        'page_table': page_table,
        'norm_mix_g': 1.0 + nrm(ks[8], (DEPTH, D_MODEL), 0.05),
        'norm_ffn_g': 1.0 + nrm(ks[9], (DEPTH, D_MODEL), 0.05),
        'norm_final_g': 1.0 + nrm(ks[10], (D_MODEL,), 0.05),
        'w_in_even': nrm(ks[11], (N_EVEN, D_MODEL, IN_EVEN), D_MODEL ** -0.5),
        'w_out_even': nrm(ks[12], (N_EVEN, W_A + W_B, D_MODEL), (W_A + W_B) ** -0.5),
        'chunk_ln_g': 1.0 + nrm(ks[13], (N_EVEN, W_B), 0.05),
        'chunk_ln_b': nrm(ks[14], (N_EVEN, W_B), 0.02),
        'chunk_w_s': nrm(ks[15], (N_EVEN, G_B, CHUNK_B, CHUNK_B), 0.5 * CHUNK_B ** -0.5),
        'chunk_b_s': 1.0 + nrm(ks[16], (N_EVEN, G_B, CHUNK_B), 0.1),
        'w_in_odd': nrm(ks[17], (N_ODD, D_MODEL, IN_ODD), D_MODEL ** -0.5),
        'fox_f_bias': 6.0 + nrm(ks[18], (N_ODD, H_C), 0.5),
        'hgrn_lb_param': nrm(ks[19], (DEPTH, H_D * DK_D), 0.5),
        'hgrn_norm_g': 1.0 + nrm(ks[20], (N_ODD, DV_D), 0.05),
        'w_out_odd': nrm(ks[21], (N_ODD, W_C + W_D, D_MODEL), (W_C + W_D) ** -0.5),
        'peer_w_q': nrm(ks[22], (DEPTH, D_MODEL, PEER_HEADS * PEER_DK), D_MODEL ** -0.5),
        'peer_sub_keys': nrm(ks[23], (DEPTH, PEER_HEADS, 2, N_KEYS, PEER_DK // 2), (PEER_DK // 2) ** -0.5),
        'peer_u': nrm(ks[24], (DEPTH, N_EXPERTS, D_MODEL), D_MODEL ** -0.5),
        'peer_v': nrm(ks[25], (DEPTH, N_EXPERTS, D_MODEL), 0.1),
    }


def reference(x_prompt, x_sample, state_ret, cache_k, cache_v, cache_logf, state_hgrn, page_table,
              norm_mix_g, norm_ffn_g, norm_final_g, w_in_even, w_out_even, chunk_ln_g, chunk_ln_b,
              chunk_w_s, chunk_b_s, w_in_odd, fox_f_bias, hgrn_lb_param, hgrn_norm_g, w_out_odd,
              peer_w_q, peer_sub_keys, peer_u, peer_v):
    past_len = page_table.shape[1] * PAGE_SIZE
    pos_p = jnp.arange(x_prompt.shape[1])
    pos_s = past_len + jnp.arange(x_sample.shape[1])
    sm = jax.nn.softmax(hgrn_lb_param.astype(jnp.float32), axis=0)
    lower_bounds = jnp.cumsum(sm, axis=0) - sm[0:1]
    hp, hs = x_prompt, x_sample
    ret_p, ret_s, cv_s = [], [], []
    kp_l, vp_l, lfp_l, ks_l, vs_l, lfs_l, hg_p, hg_s = [], [], [], [], [], [], [], []
    for layer in range(DEPTH):
        if layer % 2 == 0:
            e = layer // 2
            mp, Sp, _ = even_mix(rms_norm(hp, norm_mix_g[layer]), None, pos_p, True, w_in_even[e], w_out_even[e],
                                 chunk_ln_g[e], chunk_ln_b[e], chunk_w_s[e], chunk_b_s[e])
            ms, Ss, vn_s = even_mix(rms_norm(hs, norm_mix_g[layer]), state_ret[e], pos_s, False, w_in_even[e],
                                    w_out_even[e], chunk_ln_g[e], chunk_ln_b[e], chunk_w_s[e], chunk_b_s[e])
            ret_p.append(Sp)
            ret_s.append(Ss)
            cv_s.append(vn_s)
        else:
            o = layer // 2
            lb = lower_bounds[layer]
            mp, Sp, kp, vp, lfp = odd_mix(rms_norm(hp, norm_mix_g[layer]), None, None, page_table, True,
                                          w_in_odd[o], fox_f_bias[o], lb, hgrn_norm_g[o], w_out_odd[o])
            ms, Ss, ksn, vsn, lfs = odd_mix(rms_norm(hs, norm_mix_g[layer]), (cache_k[o], cache_v[o], cache_logf[o]),
                                            state_hgrn[o], page_table, False, w_in_odd[o], fox_f_bias[o], lb,
                                            hgrn_norm_g[o], w_out_odd[o])
            kp_l.append(kp)
            vp_l.append(vp)
            lfp_l.append(lfp)
            ks_l.append(ksn)
            vs_l.append(vsn)
            lfs_l.append(lfs)
            hg_p.append(Sp)
            hg_s.append(Ss)
        hp = hp + mp
        hs = hs + ms
        hp = hp + peer_ffn(rms_norm(hp, norm_ffn_g[layer]), peer_w_q[layer], peer_sub_keys[layer], peer_u[layer], peer_v[layer])
        hs = hs + peer_ffn(rms_norm(hs, norm_ffn_g[layer]), peer_w_q[layer], peer_sub_keys[layer], peer_u[layer], peer_v[layer])
    y_prompt = rms_norm(hp, norm_final_g)
    y_sample = rms_norm(hs, norm_final_g)
    return (y_prompt, y_sample, jnp.stack(ret_p), jnp.stack(ret_s), jnp.stack(cv_s),
            jnp.stack(kp_l), jnp.stack(vp_l), jnp.stack(lfp_l), jnp.stack(ks_l), jnp.stack(vs_l), jnp.stack(lfs_l),
            jnp.stack(hg_p), jnp.stack(hg_s))
```

```python
import functools
import math

import jax
import jax.numpy as jnp
from jax import lax
from jax.experimental import pallas as pl
from jax.experimental.pallas import tpu as pltpu

F32 = jnp.float32
BF16 = jnp.bfloat16

EPS = 1e-6
NEG_INF = -1e30
ROPE_BASE = 10000.0

D_MODEL = 2048
H_A, DK_A, DV_A = 4, D_MODEL // 8, D_MODEL // 8
G_B, DG_B = 4, D_MODEL // 8
W_A, W_B = H_A * DV_A, G_B * DG_B
H_C, HD_C = 8, D_MODEL // 16
W_C = H_C * HD_C
H_D, DK_D, DV_D = 8, D_MODEL // 16, D_MODEL // 16
W_D = H_D * DV_D
RET_CHUNK = 128
HGRN_CHUNK = 64
PAGE_SIZE = 128
N_KEYS = 128
PEER_HEADS = 8
PEER_TOPK = 16
IN_EVEN = 2 * H_A * DK_A + 2 * W_A + 2 * W_B
SAMPLE_PAD = 8

VMEM_LIMIT_BYTES = 56 * 1024 * 1024
LANES = 128


def _cparams(*sem):
    return pltpu.CompilerParams(dimension_semantics=sem, vmem_limit_bytes=VMEM_LIMIT_BYTES)


def _dot(a, b):
    return jnp.dot(a.astype(BF16), b.astype(BF16), preferred_element_type=F32)


def _dot_nt(a, b):
    return lax.dot_general(a.astype(BF16), b.astype(BF16), (((1,), (1,)), ((), ())),
                           preferred_element_type=F32)


def _dot_tn(a, b):
    return lax.dot_general(a.astype(BF16), b.astype(BF16), (((0,), (0,)), ((), ())),
                           preferred_element_type=F32)


def _split3(x):
    hi = x.astype(BF16)
    r1 = x - hi.astype(F32)
    mid = r1.astype(BF16)
    lo = (r1 - mid.astype(F32)).astype(BF16)
    return hi, mid, lo


def _dot01_lhs(m01, x):
    hi, mid, lo = _split3(x)
    d = functools.partial(jnp.dot, preferred_element_type=F32)
    return d(m01, hi) + d(m01, mid) + d(m01, lo)


def _dot01_rhs(x, m01):
    hi, mid, lo = _split3(x)
    d = functools.partial(jnp.dot, preferred_element_type=F32)
    return d(hi, m01) + d(mid, m01) + d(lo, m01)


def _log_sigmoid(x):
    return jnp.minimum(x, 0.0) - jnp.log1p(jnp.exp(-jnp.abs(x)))


def _silu(x):
    return x * jax.nn.sigmoid(x)


def _head_rms(x):
    return x * lax.rsqrt(jnp.mean(x * x, axis=-1, keepdims=True) + EPS)


def _norm_matmul_body(x_ref, g_ref, w_ref, o_ref, *rest, emit_xn):
    if emit_xn:
        xn_ref, xs_ref = rest
    else:
        (xs_ref,) = rest

    @pl.when(pl.program_id(1) == 0)
    def _():
        x = x_ref[...]
        y = x * lax.rsqrt(jnp.mean(x * x, axis=-1, keepdims=True) + EPS) * g_ref[...]
        yb = y.astype(BF16)
        xs_ref[...] = yb
        if emit_xn:
            xn_ref[...] = yb

    o_ref[...] = jnp.dot(xs_ref[...], w_ref[...], preferred_element_type=F32)


def norm_matmul(x, g, w, *, tm, tn, emit_xn=False):
    t, d = x.shape
    n = w.shape[1]
    assert t % tm == 0 and n % tn == 0
    out_shape = [jax.ShapeDtypeStruct((t, n), F32)]
    out_specs = [pl.BlockSpec((tm, tn), lambda i, j: (i, j))]
    if emit_xn:
        out_shape.append(jax.ShapeDtypeStruct((t, d), BF16))
        out_specs.append(pl.BlockSpec((tm, d), lambda i, j: (i, 0)))
    res = pl.pallas_call(
        functools.partial(_norm_matmul_body, emit_xn=emit_xn),
        grid=(t // tm, n // tn),
        in_specs=[pl.BlockSpec((tm, d), lambda i, j: (i, 0)),
                  pl.BlockSpec((1, d), lambda i, j: (0, 0)),
                  pl.BlockSpec((d, tn), lambda i, j: (0, j))],
        out_specs=out_specs,
        out_shape=out_shape,
        scratch_shapes=[pltpu.VMEM((tm, d), BF16)],
        compiler_params=_cparams("parallel", "arbitrary"),
        name="norm_matmul",
    )(x, g.reshape(1, d), w)
    return res if emit_xn else res[0]


def _matmul_res_body(*refs, n_in):
    a_refs = refs[:n_in]
    w_refs = refs[n_in:2 * n_in]
    res_ref = refs[2 * n_in]
    o_ref = refs[2 * n_in + 1]
    acc = res_ref[...]
    for a_ref, w_ref in zip(a_refs, w_refs):
        acc = acc + jnp.dot(a_ref[...], w_ref[...], preferred_element_type=F32)
    o_ref[...] = acc


def matmul_res(a_list, w_list, res, *, tm, tn):
    t, n = res.shape
    assert t % tm == 0 and n % tn == 0
    n_in = len(a_list)
    in_specs = [pl.BlockSpec((tm, a.shape[1]), lambda i, j: (i, 0)) for a in a_list]
    in_specs += [pl.BlockSpec((w.shape[0], tn), lambda i, j: (0, j)) for w in w_list]
    in_specs += [pl.BlockSpec((tm, tn), lambda i, j: (i, j))]
    return pl.pallas_call(
        functools.partial(_matmul_res_body, n_in=n_in),
        grid=(t // tm, n // tn),
        in_specs=in_specs,
        out_specs=pl.BlockSpec((tm, tn), lambda i, j: (i, j)),
        out_shape=jax.ShapeDtypeStruct((t, n), F32),
        compiler_params=_cparams("parallel", "arbitrary"),
        name="matmul_res",
    )(*a_list, *w_list, res)


def _even_mix_body(*refs, lp, lt, n_chunks, has_s0, emit_vn):
    it = iter(refs)
    z_ref, cos_ref, sin_ref, lng_ref, lnb_ref, ws_ref, bst_ref = (next(it) for _ in range(7))
    s0_ref = next(it) if has_s0 else None
    cat_ref, sout_ref = next(it), next(it)
    vn_ref = next(it) if emit_vn else None
    s_scr = next(it)

    c = pl.program_id(1)

    @pl.when(c == 0)
    def _():
        if has_s0:
            s_scr[...] = s0_ref[0]
        else:
            s_scr[...] = jnp.zeros_like(s_scr)

    t_idx = lax.broadcasted_iota(jnp.int32, (lp, lp), 0)
    s_idx = lax.broadcasted_iota(jnp.int32, (lp, lp), 1)
    causal = t_idx >= s_idx
    if lt < lp:
        causal = jnp.logical_and(causal, s_idx < lt)
    rel = jnp.maximum(t_idx - s_idx, 0).astype(F32)
    tcol = lax.broadcasted_iota(jnp.int32, (lp, 1), 0).astype(F32)
    cos = cos_ref[...]
    sin = sin_ref[...]
    half = DK_A // 2

    def rope(x):
        x1, x2 = x[:, :half], x[:, half:]
        return jnp.concatenate([x1 * cos - x2 * sin, x1 * sin + x2 * cos], axis=-1)

    ko, vo, go = H_A * DK_A, 2 * H_A * DK_A, 2 * H_A * DK_A + W_A
    for h in range(H_A):
        lg = math.log1p(-(2.0 ** (-5.0 - h)))
        q = rope(z_ref[:, h * DK_A:(h + 1) * DK_A])
        k = rope(z_ref[:, ko + h * DK_A:ko + (h + 1) * DK_A]) * DK_A ** -0.5
        v = z_ref[:, vo + h * DV_A:vo + (h + 1) * DV_A]
        g = z_ref[:, go + h * DV_A:go + (h + 1) * DV_A]
        dec = jnp.where(causal, jnp.exp(lg * rel), 0.0)
        scores = _dot_nt(q, k) * dec
        s0 = s_scr[h]
        o = _dot(scores, v) + _dot(q, s0) * jnp.exp(lg * (tcol + 1.0))
        k_dec = k * jnp.exp(lg * (lt - 1.0 - tcol))
        s_scr[h] = math.exp(lg * lt) * s0 + _dot_tn(k_dec, v)
        cat_ref[:, h * DV_A:(h + 1) * DV_A] = (_head_rms(o) * _silu(g)).astype(BF16)

    uo = go + W_A
    u = jax.nn.gelu(z_ref[:, uo:uo + W_B])
    gv = jax.nn.gelu(z_ref[:, uo + W_B:uo + 2 * W_B])
    mu = jnp.mean(gv, axis=-1, keepdims=True)
    var = jnp.mean(jnp.square(gv - mu), axis=-1, keepdims=True)
    vn = (gv - mu) * lax.rsqrt(var + EPS) * lng_ref[...] + lnb_ref[...]
    if emit_vn:
        vn_ref[...] = vn
    for gi in range(G_B):
        wm = jnp.where(causal, ws_ref[gi], 0.0)
        mix = _dot(wm, vn[:, gi * DG_B:(gi + 1) * DG_B]) + bst_ref[:, gi:gi + 1]
        cat_ref[:, W_A + gi * DG_B:W_A + (gi + 1) * DG_B] = (
            u[:, gi * DG_B:(gi + 1) * DG_B] * mix).astype(BF16)

    @pl.when(c == n_chunks - 1)
    def _():
        sout_ref[0] = s_scr[...]


def even_mix(z, cos, sin, ln_g, ln_b, w_s, b_s, s0, *, n_seq, n_chunks, lp, lt, emit_vn):
    has_s0 = s0 is not None
    rows = n_seq * n_chunks * lp
    assert z.shape == (rows, IN_EVEN)
    row_map = lambda b, c: (b * n_chunks + c, 0)
    in_specs = [pl.BlockSpec((lp, IN_EVEN), row_map),
                pl.BlockSpec((lp, DK_A // 2), lambda b, c: (c, 0)),
                pl.BlockSpec((lp, DK_A // 2), lambda b, c: (c, 0)),
                pl.BlockSpec((1, W_B), lambda b, c: (0, 0)),
                pl.BlockSpec((1, W_B), lambda b, c: (0, 0)),
                pl.BlockSpec((G_B, lp, lp), lambda b, c: (0, 0, 0)),
                pl.BlockSpec((lp, G_B), lambda b, c: (0, 0))]
    args = [z, cos, sin, ln_g.reshape(1, W_B), ln_b.reshape(1, W_B), w_s[:, :lp, :lp], b_s[:, :lp].T]
    if has_s0:
        in_specs.append(pl.BlockSpec((1, H_A, DK_A, DV_A), lambda b, c: (b, 0, 0, 0)))
        args.append(s0)
    out_shape = [jax.ShapeDtypeStruct((rows, W_A + W_B), BF16),
                 jax.ShapeDtypeStruct((n_seq, H_A, DK_A, DV_A), F32)]
    out_specs = [pl.BlockSpec((lp, W_A + W_B), row_map),
                 pl.BlockSpec((1, H_A, DK_A, DV_A), lambda b, c: (b, 0, 0, 0))]
    if emit_vn:
        out_shape.append(jax.ShapeDtypeStruct((rows, W_B), F32))
        out_specs.append(pl.BlockSpec((lp, W_B), row_map))
    return pl.pallas_call(
        functools.partial(_even_mix_body, lp=lp, lt=lt, n_chunks=n_chunks, has_s0=has_s0, emit_vn=emit_vn),
        grid=(n_seq, n_chunks),
        in_specs=in_specs,
        out_specs=out_specs,
        out_shape=out_shape,
        scratch_shapes=[pltpu.VMEM((H_A, DK_A, DV_A), F32)],
        compiler_params=_cparams("parallel", "arbitrary"),
        name="even_mix",
    )(*args)


ODD_MAIN = 4 * W_D + 3 * W_C


def _odd_mix_body(*refs, lp, lt, n_chunks, has_s0):
    it = iter(refs)
    z_ref, zf_ref, fb_ref, lb_ref, ng_ref = (next(it) for _ in range(5))
    s0_ref = next(it) if has_s0 else None
    od_ref, lf_ref, fc_ref, sout_ref = (next(it) for _ in range(4))
    s_scr, carry_scr, k_scr, b_scr, v_scr = (next(it) for _ in range(5))

    c = pl.program_id(1)

    @pl.when(c == 0)
    def _():
        if has_s0:
            s_scr[...] = s0_ref[0]
        else:
            s_scr[...] = jnp.zeros_like(s_scr)
        carry_scr[...] = jnp.zeros_like(carry_scr)

    t_idx = lax.broadcasted_iota(jnp.int32, (lp, lp), 0)
    s_idx = lax.broadcasted_iota(jnp.int32, (lp, lp), 1)
    tri = jnp.where(t_idx >= s_idx, 1.0, 0.0).astype(BF16)
    trow = lax.broadcasted_iota(jnp.int32, (lp, LANES), 0)
    valid = trow < lt

    lf = _log_sigmoid(zf_ref[...] + fb_ref[...])
    if lt < lp:
        lf = jnp.where(valid, lf, 0.0)
    lf_ref[...] = lf
    fcum = _dot01_lhs(tri, lf) + carry_scr[...]
    fc_ref[...] = fcum
    carry_scr[...] = fcum[lp - 1:lp, :]

    for h in range(H_D):
        sl = slice(h * DK_D, (h + 1) * DK_D)
        lbh = lb_ref[:, sl]
        x = z_ref[:, W_D + h * DK_D:W_D + (h + 1) * DK_D]
        logf = jnp.log(lbh + (1.0 - lbh) * jax.nn.sigmoid(x))
        if lt < lp:
            logf = jnp.where(valid, logf, 0.0)
        kd = (1.0 - lbh) * jax.nn.sigmoid(-x)
        qd = _silu(z_ref[:, sl])
        vd = z_ref[:, 2 * W_D + h * DV_D:2 * W_D + (h + 1) * DV_D]
        gd = z_ref[:, 3 * W_D + h * DV_D:3 * W_D + (h + 1) * DV_D]
        b = _dot01_lhs(tri, logf)
        s0 = s_scr[h]
        k_scr[...] = kd
        b_scr[...] = b
        v_scr[...] = vd

        def step(s, o, qd=qd, b=b):
            ks = k_scr[pl.ds(s, 1), :]
            bs = b_scr[pl.ds(s, 1), :]
            vs = v_scr[pl.ds(s, 1), :]
            p = jnp.where(trow >= s, qd * ks * jnp.exp(b - bs), 0.0)
            return o + jnp.sum(p, axis=-1, keepdims=True) * vs

        o = lax.fori_loop(0, lt, step, jnp.zeros((lp, DV_D), F32), unroll=min(lt, 8))
        o = o + _dot(qd * jnp.exp(b), s0)
        bl = b[lp - 1:lp, :]
        k_dec = kd * jnp.exp(bl - b)
        ebl = jnp.broadcast_to(jnp.exp(bl), (DV_D, DK_D)).T
        s_scr[h] = ebl * s0 + _dot_tn(k_dec, vd)
        od_ref[:, h * DV_D:(h + 1) * DV_D] = (_head_rms(o) * ng_ref[...] * _silu(gd)).astype(BF16)

    @pl.when(c == n_chunks - 1)
    def _():
        sout_ref[0] = s_scr[...]


def odd_mix(z, zf, f_bias_pad, lb, norm_g, s0, *, n_seq, n_chunks, lp, lt):
    has_s0 = s0 is not None
    rows = n_seq * n_chunks * lp
    assert z.shape == (rows, ODD_MAIN) and zf.shape == (rows, LANES)
    row_map = lambda b, c: (b * n_chunks + c, 0)
    in_specs = [pl.BlockSpec((lp, 4 * W_D), row_map),
                pl.BlockSpec((lp, LANES), row_map),
                pl.BlockSpec((1, LANES), lambda b, c: (0, 0)),
                pl.BlockSpec((1, W_D), lambda b, c: (0, 0)),
                pl.BlockSpec((1, DV_D), lambda b, c: (0, 0))]
    args = [z, zf, f_bias_pad, lb.reshape(1, W_D), norm_g.reshape(1, DV_D)]
    if has_s0:
        in_specs.append(pl.BlockSpec((1, H_D, DK_D, DV_D), lambda b, c: (b, 0, 0, 0)))
        args.append(s0)
    out_shape = [jax.ShapeDtypeStruct((rows, W_D), BF16),
                 jax.ShapeDtypeStruct((rows, LANES), F32),
                 jax.ShapeDtypeStruct((rows, LANES), F32),
                 jax.ShapeDtypeStruct((n_seq, H_D, DK_D, DV_D), F32)]
    out_specs = [pl.BlockSpec((lp, W_D), row_map),
                 pl.BlockSpec((lp, LANES), row_map),
                 pl.BlockSpec((lp, LANES), row_map),
                 pl.BlockSpec((1, H_D, DK_D, DV_D), lambda b, c: (b, 0, 0, 0))]
    return pl.pallas_call(
        functools.partial(_odd_mix_body, lp=lp, lt=lt, n_chunks=n_chunks, has_s0=has_s0),
        grid=(n_seq, n_chunks),
        in_specs=in_specs,
        out_specs=out_specs,
        out_shape=out_shape,
        scratch_shapes=[pltpu.VMEM((H_D, DK_D, DV_D), F32),
                        pltpu.VMEM((1, LANES), F32),
                        pltpu.VMEM((lp, DK_D), F32),
                        pltpu.VMEM((lp, DK_D), F32),
                        pltpu.VMEM((lp, DV_D), F32)],
        compiler_params=_cparams("parallel", "arbitrary"),
        name="odd_mix",
    )(*args)


def _fox_prompt_body(q_ref, k_ref, v_ref, fq_ref, fk_ref, o_ref, m_scr, l_scr, acc_scr, *, tq, n_k):
    i = pl.program_id(1)
    j = pl.program_id(2)

    @pl.when(j == 0)
    def _():
        m_scr[...] = jnp.full_like(m_scr, -jnp.inf)
        l_scr[...] = jnp.zeros_like(l_scr)
        acc_scr[...] = jnp.zeros_like(acc_scr)

    @pl.when(j <= i)
    def _():
        qpos = i * tq + lax.broadcasted_iota(jnp.int32, (tq, tq), 0)
        kpos = j * tq + lax.broadcasted_iota(jnp.int32, (tq, tq), 1)
        keep = qpos >= kpos
        for h in range(H_C):
            sl = slice(h * HD_C, (h + 1) * HD_C)
            s = _dot_nt(q_ref[:, sl] * HD_C ** -0.5, k_ref[:, sl])
            s = s + fq_ref[:, h:h + 1] - fk_ref[0, h:h + 1, :]
            s = jnp.where(keep, s, NEG_INF)
            m_old = m_scr[h]
            m_new = jnp.maximum(m_old, jnp.max(s, axis=-1, keepdims=True))
            corr = jnp.exp(m_old - m_new)
            p = jnp.exp(s - m_new[:, 0:1])
            l_scr[h] = l_scr[h] * corr + jnp.sum(p, axis=-1, keepdims=True)
            acc_scr[:, sl] = acc_scr[:, sl] * corr + _dot(p, v_ref[:, sl])
            m_scr[h] = m_new

    @pl.when(j == n_k - 1)
    def _():
        for h in range(H_C):
            sl = slice(h * HD_C, (h + 1) * HD_C)
            o_ref[:, sl] = (acc_scr[:, sl] / l_scr[h]).astype(BF16)


def fox_prompt(z, fcum, fcum_t, *, n_seq, seq, tq):
    n_q = seq // tq
    qcol = 4 * W_D // W_C
    return pl.pallas_call(
        functools.partial(_fox_prompt_body, tq=tq, n_k=n_q),
        grid=(n_seq, n_q, n_q),
        in_specs=[pl.BlockSpec((tq, W_C), lambda b, i, j: (b * n_q + i, qcol)),
                  pl.BlockSpec((tq, W_C), lambda b, i, j: (b * n_q + jnp.minimum(i, j), qcol + 1)),
                  pl.BlockSpec((tq, W_C), lambda b, i, j: (b * n_q + jnp.minimum(i, j), qcol + 2)),
                  pl.BlockSpec((tq, LANES), lambda b, i, j: (b * n_q + i, 0)),
                  pl.BlockSpec((1, H_C, tq), lambda b, i, j: (b, 0, jnp.minimum(i, j)))],
        out_specs=pl.BlockSpec((tq, W_C), lambda b, i, j: (b * n_q + i, 0)),
        out_shape=jax.ShapeDtypeStruct((n_seq * seq, W_C), BF16),
        scratch_shapes=[pltpu.VMEM((H_C, tq, LANES), F32),
                        pltpu.VMEM((H_C, tq, LANES), F32),
                        pltpu.VMEM((tq, W_C), F32)],
        compiler_params=_cparams("parallel", "parallel", "arbitrary"),
        name="fox_prompt",
    )(z, z, z, fcum, fcum_t)


def _fox_sample_body(pt_ref, q_ref, kn_ref, vn_ref, fn_ref, fnt_ref, ck_ref, cv_ref, clf_ref, o_ref,
                     qbd_scr, kpad_scr, vpad_scr, m_scr, l_scr, acc_scr, carry_scr, *, lt, n_pages):
    del pt_ref
    step = pl.program_id(1)
    rows = H_C * SAMPLE_PAD
    t_in = lax.broadcasted_iota(jnp.int32, (SAMPLE_PAD, PAGE_SIZE), 0)
    s_in = lax.broadcasted_iota(jnp.int32, (SAMPLE_PAD, PAGE_SIZE), 1)

    def attend(kb, vb, bias_fn, keep):
        s_all = _dot_nt(qbd_scr[...], kb)
        parts = []
        for h in range(H_C):
            s = s_all[h * SAMPLE_PAD:(h + 1) * SAMPLE_PAD, :] + bias_fn(h)
            if keep is not None:
                s = jnp.where(keep, s, NEG_INF)
            parts.append(s)
        s = jnp.concatenate(parts, axis=0)
        m_old = m_scr[...]
        m_new = jnp.maximum(m_old, jnp.max(s, axis=-1, keepdims=True))
        corr = jnp.exp(m_old - m_new)
        p = jnp.exp(s - m_new[:, 0:1])
        l_scr[...] = l_scr[...] * corr + jnp.sum(p, axis=-1, keepdims=True)
        acc_scr[...] = acc_scr[...] * corr[:, 0:1] + _dot(p, vb)
        m_scr[...] = m_new

    @pl.when(step == 0)
    def _():
        qbd_scr[...] = jnp.zeros_like(qbd_scr)
        for h in range(H_C):
            sl = slice(h * HD_C, (h + 1) * HD_C)
            qbd_scr[h * SAMPLE_PAD:(h + 1) * SAMPLE_PAD, sl] = (q_ref[:, sl] * HD_C ** -0.5).astype(BF16)
        kpad_scr[...] = jnp.zeros_like(kpad_scr)
        vpad_scr[...] = jnp.zeros_like(vpad_scr)
        kpad_scr[0:SAMPLE_PAD, :] = kn_ref[...].astype(BF16)
        vpad_scr[0:SAMPLE_PAD, :] = vn_ref[...].astype(BF16)
        m_scr[...] = jnp.full_like(m_scr, -jnp.inf)
        l_scr[...] = jnp.zeros_like(l_scr)
        acc_scr[...] = jnp.zeros_like(acc_scr)
        carry_scr[...] = jnp.zeros_like(carry_scr)
        keep = jnp.logical_and(s_in <= t_in, s_in < lt)
        attend(kpad_scr[...], vpad_scr[...],
               lambda h: fn_ref[:, h:h + 1] - fnt_ref[0, h:h + 1, :], keep)

    @pl.when(step > 0)
    def _():
        j_idx = lax.broadcasted_iota(jnp.int32, (PAGE_SIZE, PAGE_SIZE), 0)
        p_idx = lax.broadcasted_iota(jnp.int32, (PAGE_SIZE, PAGE_SIZE), 1)
        later = jnp.where(j_idx > p_idx, 1.0, 0.0).astype(BF16)
        lft = clf_ref[0]
        suffix = _dot01_rhs(lft, later) + carry_scr[...]
        carry_scr[...] = carry_scr[...] + jnp.sum(lft, axis=-1, keepdims=True)
        attend(ck_ref[0].astype(BF16), cv_ref[0].astype(BF16),
               lambda h: fn_ref[:, h:h + 1] + suffix[h:h + 1, :], None)

    @pl.when(step == n_pages)
    def _():
        for h in range(H_C):
            sl = slice(h * HD_C, (h + 1) * HD_C)
            rs = slice(h * SAMPLE_PAD, (h + 1) * SAMPLE_PAD)
            o_ref[:, sl] = (acc_scr[rs, sl] / l_scr[rs, 0:1]).astype(BF16)
    del rows


def fox_sample(z8, fn, fnt, cache_k, cache_v, cache_lft, page_table, *, lt):
    n_seq, n_pages = page_table.shape
    qcol = 4 * W_D // W_C
    rows = H_C * SAMPLE_PAD

    def page_map(b, s, pt):
        return (pt[b, n_pages - jnp.maximum(s, 1)], 0, 0)

    grid_spec = pltpu.PrefetchScalarGridSpec(
        num_scalar_prefetch=1,
        grid=(n_seq, n_pages + 1),
        in_specs=[pl.BlockSpec((SAMPLE_PAD, W_C), lambda b, s, pt: (b, qcol)),
                  pl.BlockSpec((SAMPLE_PAD, W_C), lambda b, s, pt: (b, qcol + 1)),
                  pl.BlockSpec((SAMPLE_PAD, W_C), lambda b, s, pt: (b, qcol + 2)),
                  pl.BlockSpec((SAMPLE_PAD, LANES), lambda b, s, pt: (b, 0)),
                  pl.BlockSpec((1, H_C, LANES), lambda b, s, pt: (b, 0, 0)),
                  pl.BlockSpec((1, PAGE_SIZE, W_C), page_map),
                  pl.BlockSpec((1, PAGE_SIZE, W_C), page_map),
                  pl.BlockSpec((1, H_C, PAGE_SIZE), page_map)],
        out_specs=pl.BlockSpec((SAMPLE_PAD, W_C), lambda b, s, pt: (b, 0)),
        scratch_shapes=[pltpu.VMEM((rows, W_C), BF16),
                        pltpu.VMEM((PAGE_SIZE, W_C), BF16),
                        pltpu.VMEM((PAGE_SIZE, W_C), BF16),
                        pltpu.VMEM((rows, LANES), F32),
                        pltpu.VMEM((rows, LANES), F32),
                        pltpu.VMEM((rows, W_C), F32),
                        pltpu.VMEM((H_C, LANES), F32)])
    return pl.pallas_call(
        functools.partial(_fox_sample_body, lt=lt, n_pages=n_pages),
        grid_spec=grid_spec,
        out_shape=jax.ShapeDtypeStruct((n_seq * SAMPLE_PAD, W_C), BF16),
        compiler_params=_cparams("parallel", "arbitrary"),
        name="fox_sample",
    )(page_table, z8, z8, z8, fn, fnt, cache_k, cache_v, cache_lft)


def _top_values(x, count):
    vals = []
    cur = x
    for _ in range(count):
        m = jnp.max(cur, axis=0, keepdims=True)
        vals.append(m)
        cur = jnp.where(cur == m, -jnp.inf, cur)
    return vals


def _peer_select_body(q_ref, keys_ref, thr_ref, ea_ref, s2m_ref, eb_ref):
    half = N_KEYS
    for h in range(PEER_HEADS):
        masked, tops = [], []
        for p in range(2):
            col = (h * 2 + p) * half
            st = lax.dot_general(keys_ref[h, p], q_ref[:, col:col + half], (((1,), (1,)), ((), ())),
                                 preferred_element_type=F32, precision=lax.Precision.HIGHEST)
            top = _top_values(st, PEER_TOPK)
            masked.append(jnp.where(st >= top[-1], st, -jnp.inf))
            tops.append(top)
        sv2 = jnp.concatenate(tops[1], axis=0)
        cand = jnp.concatenate([t1 + sv2 for t1 in tops[0]], axis=0)
        ctop = _top_values(cand, PEER_TOPK + 1)
        tau = 0.5 * (ctop[PEER_TOPK - 1] + ctop[PEER_TOPK])
        zsum = jnp.sum(jnp.where(cand > tau, jnp.exp(cand - (tops[0][0] + tops[1][0])), 0.0),
                       axis=0, keepdims=True)
        thr_ref[h] = tau - masked[0]
        ea_ref[h] = jnp.exp(masked[0] - tops[0][0]) / zsum
        s2m_ref[h] = masked[1]
        eb_ref[h] = jnp.exp(masked[1] - tops[1][0])


def peer_select(q, sub_keys, *, tm):
    t = q.shape[0]
    assert t % tm == 0
    out = jax.ShapeDtypeStruct((PEER_HEADS, N_KEYS, t), F32)
    spec = pl.BlockSpec((PEER_HEADS, N_KEYS, tm), lambda i: (0, 0, i))
    return pl.pallas_call(
        _peer_select_body,
        grid=(t // tm,),
        in_specs=[pl.BlockSpec((tm, q.shape[1]), lambda i: (i, 0)),
                  pl.BlockSpec(sub_keys.shape, lambda i: (0, 0, 0, 0))],
        out_specs=[spec] * 4,
        out_shape=[out] * 4,
        compiler_params=_cparams("parallel"),
        name="peer_select",
    )(q, sub_keys)


def _peer_dense_body(xn_ref, u_ref, vt_ref, thr_ref, ea_ref, s2m_ref, eb_ref, res_ref, o_ref,
                     acc_scr, hid_scr, z_scr, *, tm, te, n_e):
    e = pl.program_id(1)

    @pl.when(e == 0)
    def _():
        acc_scr[...] = jnp.zeros_like(acc_scr)

    hid_scr[...] = lax.dot_general(u_ref[...], xn_ref[...], (((1,), (1,)), ((), ())),
                                   preferred_element_type=F32)
    per = te // N_KEYS
    for ii in range(per):
        i_glob = e * per + ii
        rs = slice(ii * N_KEYS, (ii + 1) * N_KEYS)
        thr_rows = [thr_ref[h, pl.ds(i_glob, 1), :] for h in range(PEER_HEADS)]
        ea_rows = [ea_ref[h, pl.ds(i_glob, 1), :] for h in range(PEER_HEADS)]
        for lc in range(tm // LANES):
            ls = slice(lc * LANES, (lc + 1) * LANES)
            w = jnp.zeros((N_KEYS, LANES), F32)
            for h in range(PEER_HEADS):
                thr = thr_rows[h][:, ls]
                ea = ea_rows[h][:, ls]
                w = w + jnp.where(s2m_ref[h, :, ls] >= thr, eb_ref[h, :, ls] * ea, 0.0)
            z_scr[rs, ls] = (w * jax.nn.gelu(hid_scr[rs, ls])).astype(BF16)
    acc_scr[...] += jnp.dot(vt_ref[...], z_scr[...], preferred_element_type=F32)

    @pl.when(e == n_e - 1)
    def _():
        o_ref[...] = res_ref[...] + acc_scr[...].T


def peer_dense(xn, u_bf, vt_bf, sel, res, *, tm, te):
    t, d = xn.shape
    n_exp = u_bf.shape[0]
    assert t % tm == 0 and n_exp % te == 0
    n_e = n_exp // te
    gate_spec = pl.BlockSpec((PEER_HEADS, N_KEYS, tm), lambda i, e: (0, 0, i))
    return pl.pallas_call(
        functools.partial(_peer_dense_body, tm=tm, te=te, n_e=n_e),
        grid=(t // tm, n_e),
        in_specs=[pl.BlockSpec((tm, d), lambda i, e: (i, 0)),
                  pl.BlockSpec((te, d), lambda i, e: (e, 0)),
                  pl.BlockSpec((d, te), lambda i, e: (0, e)),
                  gate_spec, gate_spec, gate_spec, gate_spec,
                  pl.BlockSpec((tm, d), lambda i, e: (i, 0))],
        out_specs=pl.BlockSpec((tm, d), lambda i, e: (i, 0)),
        out_shape=jax.ShapeDtypeStruct((t, d), F32),
        scratch_shapes=[pltpu.VMEM((d, tm), F32),
                        pltpu.VMEM((te, tm), F32),
                        pltpu.VMEM((te, tm), BF16)],
        compiler_params=_cparams("parallel", "arbitrary"),
        name="peer_dense",
    )(xn, u_bf, vt_bf, *sel, res)


def _rms_body(x_ref, g_ref, o_ref):
    x = x_ref[...]
    o_ref[...] = x * lax.rsqrt(jnp.mean(x * x, axis=-1, keepdims=True) + EPS) * g_ref[...]


def rms_norm(x, g, *, tm):
    t, d = x.shape
    return pl.pallas_call(
        _rms_body,
        grid=(t // tm,),
        in_specs=[pl.BlockSpec((tm, d), lambda i: (i, 0)), pl.BlockSpec((1, d), lambda i: (0, 0))],
        out_specs=pl.BlockSpec((tm, d), lambda i: (i, 0)),
        out_shape=jax.ShapeDtypeStruct((t, d), F32),
        compiler_params=_cparams("parallel"),
        name="rms_norm",
    )(x, g.reshape(1, d))


def _rope_tables(pos):
    half = DK_A // 2
    inv = ROPE_BASE ** (-jnp.arange(half, dtype=F32) / half)
    ang = pos.astype(F32)[:, None] * inv[None, :]
    return jnp.cos(ang), jnp.sin(ang)


def _pad_rows(x, n_seq, lt):
    n = x.shape[-1]
    x = x.reshape(n_seq, lt, n)
    x = jnp.pad(x, ((0, 0), (0, SAMPLE_PAD - lt), (0, 0)))
    return x.reshape(n_seq * SAMPLE_PAD, n)


def _unpad_rows(x, n_seq, lt):
    n = x.shape[-1]
    return x.reshape(n_seq, SAMPLE_PAD, n)[:, :lt].reshape(n_seq * lt, n)


def _tiles(t):
    tm = 1024 if t % 1024 == 0 else (512 if t % 512 == 0 else LANES)
    td = 512 if t % 512 == 0 else LANES
    ts = 256 if t % 256 == 0 else LANES
    return tm, td, ts


def _peer(h, g, w_q, sub_keys, u_bf, vt_bf):
    tm, td, ts = _tiles(h.shape[0])
    q, xn = norm_matmul(h, g, w_q, tm=tm, tn=512, emit_xn=True)
    sel = peer_select(q, sub_keys, tm=ts)
    return peer_dense(xn, u_bf, vt_bf, sel, h, tm=td, te=512)


def kernel(x_prompt, x_sample, state_ret, cache_k, cache_v, cache_logf, state_hgrn, page_table,
           norm_mix_g, norm_ffn_g, norm_final_g, w_in_even, w_out_even, chunk_ln_g, chunk_ln_b,
           chunk_w_s, chunk_b_s, w_in_odd, fox_f_bias, hgrn_lb_param, hgrn_norm_g, w_out_odd,
           peer_w_q, peer_sub_keys, peer_u, peer_v):
    n_p, seq, d = x_prompt.shape
    n_s, lt, _ = x_sample.shape
    depth = norm_mix_g.shape[0]
    n_pages = page_table.shape[1]
    past_len = n_pages * PAGE_SIZE
    assert d == D_MODEL and lt <= SAMPLE_PAD and seq % RET_CHUNK == 0

    hp = x_prompt.reshape(n_p * seq, d)
    hs = x_sample.reshape(n_s * lt, d)
    tm_p, _, _ = _tiles(hp.shape[0])
    tm_s, _, _ = _tiles(hs.shape[0])

    cos_p, sin_p = _rope_tables(jnp.arange(seq))
    cos_s, sin_s = _rope_tables(past_len + jnp.arange(SAMPLE_PAD))
    sm = jax.nn.softmax(hgrn_lb_param.astype(F32), axis=0)
    lower_bounds = jnp.cumsum(sm, axis=0) - sm[0:1]

    ret_p, ret_s, cv_s = [], [], []
    kp_l, vp_l, lfp_l, ks_l, vs_l, lfs_l, hg_p, hg_s = [], [], [], [], [], [], [], []
    for layer in range(depth):
        g_mix = norm_mix_g[layer]
        if layer % 2 == 0:
            e = layer // 2
            w_in = w_in_even[e].astype(BF16)
            w_out = w_out_even[e].astype(BF16)
            zp = norm_matmul(hp, g_mix, w_in, tm=tm_p, tn=512)
            zs = norm_matmul(hs, g_mix, w_in, tm=tm_s, tn=512)
            cat_p, s_p = even_mix(zp, cos_p, sin_p, chunk_ln_g[e], chunk_ln_b[e], chunk_w_s[e], chunk_b_s[e],
                                  None, n_seq=n_p, n_chunks=seq // RET_CHUNK, lp=RET_CHUNK, lt=RET_CHUNK,
                                  emit_vn=False)
            cat_s, s_s, vn_s = even_mix(_pad_rows(zs, n_s, lt), cos_s, sin_s, chunk_ln_g[e], chunk_ln_b[e],
                                        chunk_w_s[e], chunk_b_s[e], state_ret[e], n_seq=n_s, n_chunks=1,
                                        lp=SAMPLE_PAD, lt=lt, emit_vn=True)
            hp = matmul_res([cat_p], [w_out], hp, tm=tm_p, tn=512)
            hs = matmul_res([_unpad_rows(cat_s, n_s, lt)], [w_out], hs, tm=tm_s, tn=512)
            ret_p.append(s_p)
            ret_s.append(s_s)
            cv_s.append(_unpad_rows(vn_s, n_s, lt).reshape(n_s, lt, G_B, DG_B))
        else:
            o = layer // 2
            w = w_in_odd[o]
            c0 = 3 * W_C
            w_main = jnp.concatenate([w[:, c0 + H_C:], w[:, :c0]], axis=1).astype(BF16)
            w_f = jnp.pad(w[:, c0:c0 + H_C], ((0, 0), (0, LANES - H_C))).astype(BF16)
            fb = jnp.pad(fox_f_bias[o], (0, LANES - H_C)).reshape(1, LANES)
            w_out = w_out_odd[o].astype(BF16)
            lb = lower_bounds[layer]
            zp = norm_matmul(hp, g_mix, w_main, tm=tm_p, tn=512)
            zs = norm_matmul(hs, g_mix, w_main, tm=tm_s, tn=512)
            zfp = norm_matmul(hp, g_mix, w_f, tm=tm_p, tn=LANES)
            zfs = norm_matmul(hs, g_mix, w_f, tm=tm_s, tn=LANES)
            od_p, lf_p, fc_p, sh_p = odd_mix(zp, zfp, fb, lb, hgrn_norm_g[o], None, n_seq=n_p,
                                             n_chunks=seq // HGRN_CHUNK, lp=HGRN_CHUNK, lt=HGRN_CHUNK)
            zs8 = _pad_rows(zs, n_s, lt)
            od_s, lf_s, fc_s, sh_s = odd_mix(zs8, _pad_rows(zfs, n_s, lt), fb, lb, hgrn_norm_g[o],
                                             state_hgrn[o], n_seq=n_s, n_chunks=1, lp=SAMPLE_PAD, lt=lt)
            fc_pt = fc_p.reshape(n_p, seq, LANES)[:, :, :H_C].transpose(0, 2, 1)
            oc_p = fox_prompt(zp, fc_p, fc_pt, n_seq=n_p, seq=seq, tq=min(seq, 512))
            fnt = fc_s.reshape(n_s, SAMPLE_PAD, LANES)[:, :, :H_C].transpose(0, 2, 1)
            fnt = jnp.pad(fnt, ((0, 0), (0, 0), (0, LANES - SAMPLE_PAD)))
            n_pool = cache_k.shape[1]
            oc_s = fox_sample(zs8, fc_s, fnt, cache_k[o].reshape(n_pool, PAGE_SIZE, W_C),
                              cache_v[o].reshape(n_pool, PAGE_SIZE, W_C),
                              cache_logf[o].transpose(0, 2, 1), page_table, lt=lt)
            hp = matmul_res([oc_p, od_p], [w_out[:W_C], w_out[W_C:]], hp, tm=tm_p, tn=512)
            hs = matmul_res([_unpad_rows(oc_s, n_s, lt), _unpad_rows(od_s, n_s, lt)],
                            [w_out[:W_C], w_out[W_C:]], hs, tm=tm_s, tn=512)
            kc0 = 4 * W_D + W_C
            kp_l.append(zp[:, kc0:kc0 + W_C].reshape(n_p, seq, H_C, HD_C))
            vp_l.append(zp[:, kc0 + W_C:].reshape(n_p, seq, H_C, HD_C))
            lfp_l.append(lf_p[:, :H_C].reshape(n_p, seq, H_C))
            ks_l.append(zs[:, kc0:kc0 + W_C].reshape(n_s, lt, H_C, HD_C))
            vs_l.append(zs[:, kc0 + W_C:].reshape(n_s, lt, H_C, HD_C))
            lfs_l.append(_unpad_rows(lf_s, n_s, lt)[:, :H_C].reshape(n_s, lt, H_C))
            hg_p.append(sh_p)
            hg_s.append(sh_s)
        w_q = peer_w_q[layer].astype(BF16)
        u_bf = peer_u[layer].astype(BF16)
        vt_bf = peer_v[layer].astype(BF16).T
        hp = _peer(hp, norm_ffn_g[layer], w_q, peer_sub_keys[layer], u_bf, vt_bf)
        hs = _peer(hs, norm_ffn_g[layer], w_q, peer_sub_keys[layer], u_bf, vt_bf)
    y_p = rms_norm(hp, norm_final_g, tm=tm_p).reshape(n_p, seq, d)
    y_s = rms_norm(hs, norm_final_g, tm=tm_s).reshape(n_s, lt, d)
    return (y_p, y_s, jnp.stack(ret_p), jnp.stack(ret_s), jnp.stack(cv_s),
            jnp.stack(kp_l), jnp.stack(vp_l), jnp.stack(lfp_l), jnp.stack(ks_l), jnp.stack(vs_l),
            jnp.stack(lfs_l), jnp.stack(hg_p), jnp.stack(hg_s))
```
